```python
import math
import jax, jax.numpy as jnp
from jax import lax
import numpy as np

D_MODEL = 2048
BATCH = 4
SEQ = 4096
DEPTH = 4

GRID_W = 64
CTX_LEN = 256
N_MOD = 9
D_FF = 5504
EPS = 1e-6
FOURIER_WIDTH = D_MODEL // 4
FOURIER_GROUPS = 4
FOURIER_GROUP_DIM = FOURIER_WIDTH // FOURIER_GROUPS
DN_HEAD_DIM = 128
DN_HEADS = (3 * D_MODEL // 8) // DN_HEAD_DIM
DN_WIDTH = DN_HEADS * DN_HEAD_DIM
DN_CHUNK = 64
DN_CONV = 5
DIFF_V_DIM = 128
DIFF_QK_DIM = DIFF_V_DIM // 2
DIFF_HEADS = (D_MODEL - FOURIER_WIDTH - DN_WIDTH) // DIFF_V_DIM
DIFF_WIDTH = DIFF_HEADS * DIFF_V_DIM
Q_BLOCK = 128
ROPE_BASE = 10000.0
MIX_WIDTH = FOURIER_WIDTH + DN_WIDTH + DIFF_WIDTH
IN_SIZES = (FOURIER_WIDTH, 3 * DN_WIDTH, DN_WIDTH, 2 * DN_HEADS, 2 * DN_HEADS, DIFF_WIDTH, DIFF_WIDTH, DIFF_WIDTH)
IN_WIDTH = sum(IN_SIZES)

kernel_name = "hybrid_fourier_deltanet_diffattn_macaron"


def rms_norm(x, gain):
    xf = x.astype(jnp.float32)
    y = xf * lax.rsqrt(jnp.mean(xf * xf, axis=-1, keepdims=True) + EPS)
    return (y * gain.astype(jnp.float32)).astype(x.dtype)


def l2norm(x):
    return x * lax.rsqrt(jnp.sum(x * x, axis=-1, keepdims=True) + EPS)


def modulate(h, gain, mod, i):
    return rms_norm(h, gain) * (1 + mod[:, 3 * i + 1, None, :]) + mod[:, 3 * i, None, :]


def add_residual(h, out, gain, mod, i, weight):
    return h + weight * mod[:, 3 * i + 2, None, :] * rms_norm(out, gain)


def swiglu(y, w_in, w_out):
    gate, up = jnp.split(y @ w_in, 2, axis=-1)
    return (jax.nn.silu(gate) * up) @ w_out


def split_cols(p):
    out, start = [], 0
    for size in IN_SIZES:
        out.append(p[..., start:start + size])
        start += size
    return out


def centred_dwconv(x, w):
    k, ch = w.shape
    return lax.conv_general_dilated(x, w[:, None, :].astype(x.dtype), window_strides=(1,),
                                    padding=[(k // 2, k // 2)], dimension_numbers=("NWC", "WIO", "NWC"),
                                    feature_group_count=ch)


def axial_rope_tables(rows, dim):
    r, col = jnp.meshgrid(jnp.arange(rows), jnp.arange(GRID_W), indexing="ij")
    r = r.reshape(-1).astype(jnp.float32)
    col = col.reshape(-1).astype(jnp.float32)
    axis_dim = dim // 2
    inv_freq = 1.0 / (ROPE_BASE ** (jnp.arange(0, axis_dim, 2, dtype=jnp.float32) / axis_dim))
    ar = r[:, None] * inv_freq
    ac = col[:, None] * inv_freq
    ang = jnp.concatenate([ar, ar, ac, ac], axis=-1)
    return jnp.cos(ang), jnp.sin(ang)


def _rotate_half(u):
    u1, u2 = jnp.split(u, 2, axis=-1)
    return jnp.concatenate([-u2, u1], axis=-1)


def apply_axial_rope(x, cos, sin):
    xr, xc = jnp.split(x, 2, axis=-1)
    rot = jnp.concatenate([_rotate_half(xr), _rotate_half(xc)], axis=-1)
    cos = cos[None, :, None, None, :].astype(x.dtype)
    sin = sin[None, :, None, None, :].astype(x.dtype)
    return x * cos + rot * sin


def fourier_mix(u):
    b, n, _ = u.shape
    uf = u.astype(jnp.float32).reshape(b, n, FOURIER_GROUPS, FOURIER_GROUP_DIM)
    y = jnp.fft.fftn(uf, axes=(1, 3), norm="ortho").real
    return y.reshape(b, n, FOURIER_WIDTH).astype(u.dtype)


def dn_prepare(qkv, a, bt, a_log, dt_bias):
    bsz, n, _ = qkv.shape
    q, k, v = jnp.split(qkv.astype(jnp.float32), 3, axis=-1)
    heads = lambda t: t.reshape(bsz, n, DN_HEADS, DN_HEAD_DIM).transpose(0, 2, 1, 3)
    q = l2norm(heads(q)) * (DN_HEAD_DIM ** -0.5)
    k = l2norm(heads(k))
    v = heads(v)
    dirs = lambda t: t.astype(jnp.float32).reshape(bsz, n, 2, DN_HEADS).transpose(2, 0, 3, 1)
    g = -jnp.exp(a_log.astype(jnp.float32))[:, None, :, None] * jax.nn.softplus(
        dirs(a) + dt_bias.astype(jnp.float32)[:, None, :, None])
    beta = jax.nn.sigmoid(dirs(bt))
    return q, k, v, g, beta


def gated_delta_chunked(q, k, v, g, beta, s0):
    b, h, n, _ = q.shape
    dv = v.shape[-1]
    nc = n // DN_CHUNK
    rs = lambda t: t.reshape(b, h, nc, DN_CHUNK, *t.shape[3:])
    q, k, v, g, beta = rs(q), rs(k), rs(v), rs(g), rs(beta)
    g = jnp.cumsum(g, axis=-1)
    causal = jnp.tril(jnp.ones((DN_CHUNK, DN_CHUNK), bool))
    strict = jnp.tril(jnp.ones((DN_CHUNK, DN_CHUNK), bool), -1)
    diff = g[..., :, None] - g[..., None, :]
    decay = jnp.where(causal, jnp.exp(jnp.where(causal, diff, 0.0)), 0.0)
    k_beta = k * beta[..., None]
    low = jnp.where(strict, jnp.einsum("bhncd,bhnsd->bhncs", k_beta, k) * decay, 0.0)
    rhs = jnp.concatenate([v * beta[..., None], k_beta * jnp.exp(g)[..., None]], axis=-1)
    sol = lax.linalg.triangular_solve(low, rhs, left_side=True, lower=True, unit_diagonal=True)
    u, w = sol[..., :dv], sol[..., dv:]
    intra = jnp.where(causal, jnp.einsum("bhncd,bhnsd->bhncs", q, k) * decay, 0.0)

    def step(s, xs):
        q_c, k_c, u_c, w_c, g_c, a_c = xs
        v_new = u_c - jnp.einsum("bhcd,bhde->bhce", w_c, s)
        o = (jnp.einsum("bhcd,bhde->bhce", q_c * jnp.exp(g_c)[..., None], s)
             + jnp.einsum("bhcs,bhse->bhce", a_c, v_new))
        g_last = g_c[..., -1]
        s = (s * jnp.exp(g_last)[..., None, None]
             + jnp.einsum("bhcd,bhce->bhde", k_c * jnp.exp(g_last[..., None] - g_c)[..., None], v_new))
        return s, o

    xs = tuple(jnp.moveaxis(t, 2, 0) for t in (q, k, u, w, g, intra))
    s, o = lax.scan(step, s0, xs)
    return jnp.moveaxis(o, 0, 2).reshape(b, h, n, dv), s


def bidir_delta(ctx_f, lat_f):
    qc, kc, vc, gc, bc = ctx_f
    ql, kl, vl, gl, bl = lat_f
    s0 = jnp.zeros(qc.shape[:2] + (DN_HEAD_DIM, DN_HEAD_DIM), jnp.float32)
    fl = lambda t: jnp.flip(t, axis=2)
    oc_f, sc_f = gated_delta_chunked(qc, kc, vc, gc[0], bc[0], s0)
    ol_f, _ = gated_delta_chunked(ql, kl, vl, gl[0], bl[0], sc_f)
    oc_b, sc_b = gated_delta_chunked(fl(qc), fl(kc), fl(vc), fl(gc[1]), fl(bc[1]), s0)
    ol_b, _ = gated_delta_chunked(fl(ql), fl(kl), fl(vl), fl(gl[1]), fl(bl[1]), sc_b)
    return oc_f + fl(oc_b), ol_f + fl(ol_b)


def dn_output(o, gate, gain, dtype):
    bsz, _, n, _ = o.shape
    o = rms_norm(o.transpose(0, 2, 1, 3), gain)
    gt = jax.nn.silu(gate.astype(jnp.float32)).reshape(bsz, n, DN_HEADS, DN_HEAD_DIM)
    return (o * gt).reshape(bsz, n, DN_WIDTH).astype(dtype)


def diff_attend(q, k, v, lam):
    s = jnp.einsum("bqhmd,bkhmd->bhmqk", q, k).astype(jnp.float32) * (DIFF_QK_DIM ** -0.5)
    p = jax.nn.softmax(s, axis=-1)
    wts = p[:, :, 0] - lam * p[:, :, 1]
    return jnp.einsum("bhqk,bkhe->bqhe", wts.astype(v.dtype), v)


def diff_attention_blocked(q, k, v, lam):
    b, n, h, m, d = q.shape
    nb = n // Q_BLOCK
    qb = jnp.moveaxis(q.reshape(b, nb, Q_BLOCK, h, m, d), 1, 0)
    ob = lax.map(lambda qi: diff_attend(qi, k, v, lam), qb)
    return jnp.moveaxis(ob, 0, 1).reshape(b, n, h, v.shape[-1])


def diff_output(o, gain, lam_init):
    bsz, n = o.shape[:2]
    return (rms_norm(o, gain) * (1.0 - lam_init)).reshape(bsz, n, DIFF_WIDTH)


def token_mixer(y, yc, w_in, w_out, dn_conv, dn_a_log, dn_dt_bias, dn_norm, diff_lambda, diff_norm,
                lam_init, cos, sin, with_ctx):
    f_l, qkv_l, dg_l, da_l, db_l, cq_l, ck_l, cv_l = split_cols(y @ w_in)
    f_c, qkv_c, dg_c, da_c, db_c, cq_c, ck_c, cv_c = split_cols(yc @ w_in)
    dn_l = dn_prepare(jax.nn.silu(centred_dwconv(qkv_l, dn_conv)), da_l, db_l, dn_a_log, dn_dt_bias)
    dn_c = dn_prepare(jax.nn.silu(centred_dwconv(qkv_c, dn_conv)), da_c, db_c, dn_a_log, dn_dt_bias)
    o_dn_c, o_dn_l = bidir_delta(dn_c, dn_l)
    lf = diff_lambda.astype(jnp.float32)
    lam = jnp.exp(jnp.sum(lf[0] * lf[1])) - jnp.exp(jnp.sum(lf[2] * lf[3])) + lam_init
    qk_heads = lambda t: t.reshape(t.shape[0], t.shape[1], DIFF_HEADS, 2, DIFF_QK_DIM)
    v_heads = lambda t: t.reshape(t.shape[0], t.shape[1], DIFF_HEADS, DIFF_V_DIM)
    q_l = apply_axial_rope(qk_heads(cq_l), cos, sin)
    k_l = apply_axial_rope(qk_heads(ck_l), cos, sin)
    q_c, k_c, v_c = qk_heads(cq_c), qk_heads(ck_c), v_heads(cv_c)
    k_all = jnp.concatenate([k_c, k_l], axis=1)
    v_all = jnp.concatenate([v_c, v_heads(cv_l)], axis=1)
    att_l = diff_attention_blocked(q_l, k_all, v_all, lam)
    out_l = jnp.concatenate([fourier_mix(f_l), dn_output(o_dn_l, dg_l, dn_norm, y.dtype),
                             diff_output(att_l, diff_norm, lam_init)], axis=-1) @ w_out
    if not with_ctx:
        return out_l, None
    att_c = diff_attend(q_c, k_c, v_c, lam)
    out_c = jnp.concatenate([fourier_mix(f_c), dn_output(o_dn_c, dg_c, dn_norm, yc.dtype),
                             diff_output(att_c, diff_norm, lam_init)], axis=-1) @ w_out
    return out_l, out_c


def setup_inputs(seed: int = 0) -> dict:
    key = jax.random.key(seed)
    ks = jax.random.split(key, 20)
    f32 = jnp.float32
    nrm = lambda k, shape, scale: jax.random.normal(k, shape, f32) * scale
    dt = jnp.exp(jax.random.uniform(ks[13], (DEPTH, 2, DN_HEADS), f32, math.log(0.001), math.log(0.1)))
    return {
        "x": nrm(ks[0], (BATCH, SEQ, D_MODEL), 1.0),
        "c": nrm(ks[1], (BATCH, D_MODEL), 1.0),
        "ctx": nrm(ks[2], (BATCH, CTX_LEN, D_MODEL), 1.0),
        "c_ctx": nrm(ks[3], (D_MODEL,), 1.0),
        "w_ada": nrm(ks[4], (DEPTH, D_MODEL, N_MOD * D_MODEL), 0.5 * D_MODEL ** -0.5),
        "b_ada": nrm(ks[5], (DEPTH, N_MOD * D_MODEL), 0.02),
        "norm_pre": 1.0 + nrm(ks[6], (DEPTH, 3, D_MODEL), 0.02),
        "norm_post": 1.0 + nrm(ks[7], (DEPTH, 3, D_MODEL), 0.02),
        "ffn_w_in": nrm(ks[8], (DEPTH, 2, D_MODEL, 2 * D_FF), D_MODEL ** -0.5),
        "ffn_w_out": nrm(ks[9], (DEPTH, 2, D_FF, D_MODEL), D_FF ** -0.5),
        "w_in": nrm(ks[10], (DEPTH, D_MODEL, IN_WIDTH), D_MODEL ** -0.5),
        "dn_conv": nrm(ks[11], (DEPTH, DN_CONV, 3 * DN_WIDTH), DN_CONV ** -0.5),
        "dn_a_log": jnp.log(jax.random.uniform(ks[12], (DEPTH, 2, DN_HEADS), f32, 1.0, 16.0)),
        "dn_dt_bias": dt + jnp.log(-jnp.expm1(-dt)),
        "dn_norm": 1.0 + nrm(ks[14], (DEPTH, DN_HEAD_DIM), 0.02),
        "diff_lambda": nrm(ks[15], (DEPTH, 4, DIFF_QK_DIM), 0.1),
        "diff_norm": 1.0 + nrm(ks[16], (DEPTH, DIFF_V_DIM), 0.02),
        "w_out": nrm(ks[17], (DEPTH, MIX_WIDTH, D_MODEL), MIX_WIDTH ** -0.5),
    }


def reference(x, c, ctx, c_ctx, w_ada, b_ada, norm_pre, norm_post, ffn_w_in, ffn_w_out, w_in, dn_conv,
              dn_a_log, dn_dt_bias, dn_norm, diff_lambda, diff_norm, w_out):
    n_b, n_lat, _ = x.shape
    rows = n_lat // GRID_W
    cos, sin = axial_rope_tables(rows, DIFF_QK_DIM)
    silu_c = jax.nn.silu(c)
    silu_cc = jax.nn.silu(c_ctx)
    h, hc = x, ctx
    for layer in range(DEPTH):
        last = layer == DEPTH - 1
        lam_init = 0.8 - 0.6 * math.exp(-0.3 * layer)
        mod = (silu_c @ w_ada[layer] + b_ada[layer]).reshape(n_b, N_MOD, D_MODEL)
        modc = (silu_cc @ w_ada[layer] + b_ada[layer]).reshape(1, N_MOD, D_MODEL)
        wi1, wo1 = ffn_w_in[layer, 0], ffn_w_out[layer, 0]
        wi2, wo2 = ffn_w_in[layer, 1], ffn_w_out[layer, 1]
        h = add_residual(h, swiglu(modulate(h, norm_pre[layer, 0], mod, 0), wi1, wo1), norm_post[layer, 0], mod, 0, 0.5)
        hc = add_residual(hc, swiglu(modulate(hc, norm_pre[layer, 0], modc, 0), wi1, wo1), norm_post[layer, 0], modc, 0, 0.5)
        m, mc = token_mixer(modulate(h, norm_pre[layer, 1], mod, 1), modulate(hc, norm_pre[layer, 1], modc, 1),
                            w_in[layer], w_out[layer], dn_conv[layer], dn_a_log[layer], dn_dt_bias[layer],
                            dn_norm[layer], diff_lambda[layer], diff_norm[layer], lam_init, cos, sin, not last)
        h = add_residual(h, m, norm_post[layer, 1], mod, 1, 1.0)
        h = add_residual(h, swiglu(modulate(h, norm_pre[layer, 2], mod, 2), wi2, wo2), norm_post[layer, 2], mod, 2, 0.5)
        if not last:
            hc = add_residual(hc, mc, norm_post[layer, 1], modc, 1, 1.0)
            hc = add_residual(hc, swiglu(modulate(hc, norm_pre[layer, 2], modc, 2), wi2, wo2), norm_post[layer, 2], modc, 2, 0.5)
    return h
```

```python
import functools
import math

import numpy as np
import jax
import jax.numpy as jnp
from jax import lax
from jax.experimental import pallas as pl
from jax.experimental.pallas import tpu as pltpu

F32 = jnp.float32
BF16 = jnp.bfloat16

D_MODEL = 2048
BATCH = 4
SEQ = 4096
DEPTH = 4
GRID_W = 64
CTX_LEN = 256
N_MOD = 9
D_FF = 5504
EPS = 1e-6
FOURIER_WIDTH = 512
FOURIER_GROUP_DIM = 128
DN_HEAD_DIM = 128
DN_HEADS = 6
DN_WIDTH = 768
DN_CHUNK = 64
DN_CONV = 5
DIFF_V_DIM = 128
DIFF_QK_DIM = 64
DIFF_HEADS = 6
DIFF_WIDTH = 768
ROPE_BASE = 10000.0

LAT_ROWS = BATCH * SEQ
CTX_ROWS = BATCH * CTX_LEN
ROWS = LAT_ROWS + CTX_ROWS

LANE = 128
ROW_TILE = 512
FF_TILE = 512
FF_PAD = 5632
MAIN_WIDTH = 5632
PROJ_TILE = 512
QKV_BLK768 = 0
CQ_BLK128 = 18
CK_BLK128 = 24
CV_BLK128 = 30
DG_BLK768 = 6
A_BLK128 = 42
BT_BLK128 = 43
ATT_Q_TILE = 256
FFT_R = 64
FFT_G = 8
VMEM_LIMIT = 56 * 1024 * 1024

HI = lax.Precision.HIGHEST


def _cparams(sem):
    return pltpu.CompilerParams(dimension_semantics=sem, vmem_limit_bytes=VMEM_LIMIT)


def _rms(x):
    return x * lax.rsqrt(jnp.mean(x * x, axis=-1, keepdims=True) + EPS)


def _silu(x):
    return x * jax.nn.sigmoid(x)


def _mod_row(i, tile):
    return jnp.where(i < LAT_ROWS // tile, i // (SEQ // tile), BATCH)


def _mod_spec(k, tile):
    return pl.BlockSpec((None, None, 1, D_MODEL), lambda i, j, k=k: (_mod_row(i, tile), k, 0, 0))


def _ada_kernel(c_ref, w_ref, b_ref, out_ref):
    x = _silu(c_ref[...]).astype(BF16)
    out_ref[...] = jnp.dot(x, w_ref[...].astype(BF16), preferred_element_type=F32) + b_ref[...]


def _ada(c8, w_ada, b_ada):
    tn = 1024
    nt = N_MOD * D_MODEL // tn
    return pl.pallas_call(
        _ada_kernel,
        grid=(DEPTH, nt),
        in_specs=[
            pl.BlockSpec((8, D_MODEL), lambda l, j: (0, 0)),
            pl.BlockSpec((None, D_MODEL, tn), lambda l, j: (l, 0, j)),
            pl.BlockSpec((None, 1, tn), lambda l, j: (l, 0, j)),
        ],
        out_specs=pl.BlockSpec((None, 8, tn), lambda l, j: (l, 0, j)),
        out_shape=jax.ShapeDtypeStruct((DEPTH, 8, N_MOD * D_MODEL), F32),
        compiler_params=_cparams(("arbitrary", "arbitrary")),
        name="ada",
    )(c8, w_ada, b_ada.reshape(DEPTH, 1, N_MOD * D_MODEL))


def _ffn_kernel(h_ref, shift_ref, scale_ref, gate_ref, gpre_ref, gpost_ref, wgu_ref, wo_ref,
                out_ref, y_scr, acc_scr):
    j = pl.program_id(1)

    @pl.when(j == 0)
    def _():
        y = _rms(h_ref[...]) * gpre_ref[...] * (1.0 + scale_ref[...]) + shift_ref[...]
        y_scr[...] = y.astype(BF16)
        acc_scr[...] = jnp.zeros_like(acc_scr)

    gu = jnp.dot(y_scr[...], wgu_ref[...], preferred_element_type=F32)
    a = _silu(gu[:, :FF_TILE]) * gu[:, FF_TILE:]
    acc_scr[...] += jnp.dot(a.astype(BF16), wo_ref[...], preferred_element_type=F32)

    @pl.when(j == pl.num_programs(1) - 1)
    def _():
        out_ref[...] = h_ref[...] + 0.5 * gate_ref[...] * (_rms(acc_scr[...]) * gpost_ref[...])


def _ffn(h, mod, gpre, gpost, wgu, wo, sub):
    tm = ROW_TILE
    row = pl.BlockSpec((tm, D_MODEL), lambda i, j: (i, 0))
    vec = pl.BlockSpec((1, D_MODEL), lambda i, j: (0, 0))
    return pl.pallas_call(
        _ffn_kernel,
        grid=(ROWS // tm, FF_PAD // FF_TILE),
        in_specs=[
            row, _mod_spec(3 * sub, tm), _mod_spec(3 * sub + 1, tm), _mod_spec(3 * sub + 2, tm), vec, vec,
            pl.BlockSpec((D_MODEL, 2 * FF_TILE), lambda i, j: (0, j)),
            pl.BlockSpec((FF_TILE, D_MODEL), lambda i, j: (j, 0)),
        ],
        out_specs=row,
        out_shape=jax.ShapeDtypeStruct((ROWS, D_MODEL), F32),
        scratch_shapes=[pltpu.VMEM((tm, D_MODEL), BF16), pltpu.VMEM((tm, D_MODEL), F32)],
        compiler_params=_cparams(("arbitrary", "arbitrary")),
        name="ffn",
    )(h, mod, mod, mod, gpre, gpost, wgu, wo)


def _proj_kernel(h_ref, shift_ref, scale_ref, gpre_ref, w_ref, main_ref, f_ref, y_scr):
    j = pl.program_id(1)

    @pl.when(j == 0)
    def _():
        y = _rms(h_ref[...]) * gpre_ref[...] * (1.0 + scale_ref[...]) + shift_ref[...]
        y_scr[...] = y.astype(BF16)

    res = jnp.dot(y_scr[...], w_ref[...], preferred_element_type=F32)
    n_main = MAIN_WIDTH // PROJ_TILE

    @pl.when(j < n_main)
    def _():
        main_ref[...] = res

    @pl.when(j == n_main)
    def _():
        f_ref[...] = res


def _proj(h, mod, gpre, w):
    tm = ROW_TILE
    n_main = MAIN_WIDTH // PROJ_TILE
    return pl.pallas_call(
        _proj_kernel,
        grid=(ROWS // tm, n_main + 1),
        in_specs=[
            pl.BlockSpec((tm, D_MODEL), lambda i, j: (i, 0)),
            _mod_spec(3, tm), _mod_spec(4, tm),
            pl.BlockSpec((1, D_MODEL), lambda i, j: (0, 0)),
            pl.BlockSpec((D_MODEL, PROJ_TILE), lambda i, j: (0, j)),
        ],
        out_specs=[
            pl.BlockSpec((tm, PROJ_TILE), lambda i, j: (i, jnp.minimum(j, n_main - 1))),
            pl.BlockSpec((tm, FOURIER_WIDTH), lambda i, j: (i, 0)),
        ],
        out_shape=[jax.ShapeDtypeStruct((ROWS, MAIN_WIDTH), F32),
                   jax.ShapeDtypeStruct((ROWS, FOURIER_WIDTH), F32)],
        scratch_shapes=[pltpu.VMEM((tm, D_MODEL), BF16)],
        compiler_params=_cparams(("arbitrary", "arbitrary")),
        name="proj",
    )(h, mod, mod, gpre, w)


def _dft_mats(n):
    k = np.arange(n)
    ang = 2.0 * np.pi * ((k[:, None] * k[None, :]) % n) / n
    return np.cos(ang), np.sin(ang)


def _fourier_consts():
    c64, s64 = _dft_mats(FFT_R)
    stage_a = np.concatenate([c64, -s64], axis=0)
    stage_b = np.block([[c64, s64], [-s64, c64]])
    k2 = np.arange(FFT_R)[:, None]
    n1 = np.arange(FFT_R)[None, :]
    ang = 2.0 * np.pi * (k2 * n1) / (FFT_R * FFT_R)
    tw_c, tw_s = np.cos(ang), np.sin(ang)
    cg, sg = _dft_mats(FOURIER_GROUP_DIM)
    eye = np.eye(FOURIER_WIDTH // FOURIER_GROUP_DIM)
    cbd, sbd = np.kron(eye, cg), np.kron(eye, sg)
    f32 = lambda a: jnp.asarray(a, F32)
    return dict(stage_a=f32(stage_a), stage_b=f32(stage_b), tw_c=f32(tw_c), tw_s=f32(tw_s),
                cbd=cbd, sbd=sbd)


def _fft_a_kernel(x_ref, fa_ref, twc_ref, tws_ref, out_ref):
    a = jnp.dot(fa_ref[...], x_ref[...], precision=HI, preferred_element_type=F32)
    ar, ai = a[:FFT_R], a[FFT_R:]
    tc, ts = twc_ref[...], tws_ref[...]
    br = ar * tc + ai * ts
    bi = ai * tc - ar * ts
    for g in range(FFT_G):
        sl = slice(g * FOURIER_WIDTH, (g + 1) * FOURIER_WIDTH)
        out_ref[0, g] = br[:, sl]
        out_ref[1, g] = bi[:, sl]


def _fft_b_kernel(x_ref, fb_ref, cs_ref, out_ref):
    y = jnp.dot(fb_ref[...], x_ref[...], precision=HI, preferred_element_type=F32)
    yr, yi = y[:FFT_R], y[FFT_R:]
    for g in range(FFT_G):
        sl = slice(g * FOURIER_WIDTH, (g + 1) * FOURIER_WIDTH)
        yy = jnp.concatenate([yr[:, sl], yi[:, sl]], axis=1)
        out_ref[:, sl] = jnp.dot(yy, cs_ref[...], precision=HI, preferred_element_type=F32)


def _fourier_lat(f, fc):
    wide = FFT_R * FOURIER_WIDTH
    gw = FFT_G * FOURIER_WIDTH
    x2 = f.reshape(ROWS // FFT_R, wide)
    scale = 1.0 / math.sqrt(SEQ * FOURIER_GROUP_DIM)
    twc = jnp.repeat(fc["tw_c"], FOURIER_WIDTH, axis=1)
    tws = jnp.repeat(fc["tw_s"], FOURIER_WIDTH, axis=1)
    cs = jnp.asarray(np.concatenate([fc["cbd"], fc["sbd"]], axis=0) * scale, F32)
    mid = pl.pallas_call(
        _fft_a_kernel,
        grid=(BATCH, FFT_R // FFT_G),
        in_specs=[
            pl.BlockSpec((FFT_R, gw), lambda b, i: (b, i)),
            pl.BlockSpec((2 * FFT_R, FFT_R), lambda b, i: (0, 0)),
            pl.BlockSpec((FFT_R, gw), lambda b, i: (0, i)),
            pl.BlockSpec((FFT_R, gw), lambda b, i: (0, i)),
        ],
        out_specs=pl.BlockSpec((None, 2, FFT_G, FFT_R, FOURIER_WIDTH), lambda b, i: (b, 0, i, 0, 0)),
        out_shape=jax.ShapeDtypeStruct((BATCH, 2, FFT_R, FFT_R, FOURIER_WIDTH), F32),
        compiler_params=_cparams(("arbitrary", "arbitrary")),
        name="fft_a",
    )(x2, fc["stage_a"], twc, tws)
    mid = mid.reshape(BATCH, 2 * FFT_R, wide)
    out = pl.pallas_call(
        _fft_b_kernel,
        grid=(BATCH, FFT_R // FFT_G),
        in_specs=[
            pl.BlockSpec((None, 2 * FFT_R, gw), lambda b, i: (b, 0, i)),
            pl.BlockSpec((2 * FFT_R, 2 * FFT_R), lambda b, i: (0, 0)),
            pl.BlockSpec((2 * FOURIER_WIDTH, FOURIER_WIDTH), lambda b, i: (0, 0)),
        ],
        out_specs=pl.BlockSpec((FFT_R, gw), lambda b, i: (b, i)),
        out_shape=jax.ShapeDtypeStruct((LAT_ROWS // FFT_R, wide), F32),
        compiler_params=_cparams(("arbitrary", "arbitrary")),
        name="fft_b",
    )(mid, fc["stage_b"], cs)
    return out.reshape(LAT_ROWS, FOURIER_WIDTH)


def _fourier_ctx_kernel(x_ref, cn_ref, sn_ref, cbd_ref, sbd_ref, out_ref):
    x = x_ref[...]
    p = jnp.dot(x, cbd_ref[...], precision=HI, preferred_element_type=F32)
    q = jnp.dot(x, sbd_ref[...], precision=HI, preferred_element_type=F32)
    out_ref[...] = (jnp.dot(cn_ref[...], p, precision=HI, preferred_element_type=F32)
                    - jnp.dot(sn_ref[...], q, precision=HI, preferred_element_type=F32))


def _fourier_ctx(f, fc):
    cn, sn = _dft_mats(CTX_LEN)
    scale = 1.0 / math.sqrt(CTX_LEN * FOURIER_GROUP_DIM)
    const = lambda shape: pl.BlockSpec(shape, lambda b: (0, 0))
    return pl.pallas_call(
        _fourier_ctx_kernel,
        grid=(BATCH,),
        in_specs=[
            pl.BlockSpec((CTX_LEN, FOURIER_WIDTH), lambda b: (LAT_ROWS // CTX_LEN + b, 0)),
            const((CTX_LEN, CTX_LEN)), const((CTX_LEN, CTX_LEN)),
            const((FOURIER_WIDTH, FOURIER_WIDTH)), const((FOURIER_WIDTH, FOURIER_WIDTH)),
        ],
        out_specs=pl.BlockSpec((CTX_LEN, FOURIER_WIDTH), lambda b: (b, 0)),
        out_shape=jax.ShapeDtypeStruct((CTX_ROWS, FOURIER_WIDTH), F32),
        compiler_params=_cparams(("arbitrary",)),
        name="fourier_ctx",
    )(f, jnp.asarray(cn, F32), jnp.asarray(sn, F32),
      jnp.asarray(fc["cbd"] * scale, F32), jnp.asarray(fc["sbd"] * scale, F32))


def _dn_prep_kernel(x_ref, w_ref, out_ref, pad_scr, *, n):
    j = pl.program_id(1)
    half = DN_CONV // 2
    pad_scr[0:8, :] = jnp.zeros((8, LANE), F32)
    pad_scr[8 + n:16 + n, :] = jnp.zeros((8, LANE), F32)
    pad_scr[8:8 + n, :] = x_ref[...]
    w = w_ref[...]
    is_qk = j < 2 * DN_HEADS
    post = jnp.where(j < DN_HEADS, DN_HEAD_DIM ** -0.5, 1.0).astype(F32)
    rc = 256
    for c in range(n // rc):
        acc = jnp.zeros((rc, LANE), F32)
        for t in range(DN_CONV):
            acc = acc + pad_scr[pl.ds(8 + c * rc + t - half, rc), :] * w[t:t + 1, :]
        y = _silu(acc)
        nrm = y * lax.rsqrt(jnp.sum(y * y, axis=-1, keepdims=True) + EPS) * post
        out_ref[c * rc:(c + 1) * rc, :] = jnp.where(is_qk, nrm, y)


def _dn_prep(main, conv_w, n, row_blk0):
    nblk = 3 * DN_WIDTH // LANE
    return pl.pallas_call(
        functools.partial(_dn_prep_kernel, n=n),
        grid=(BATCH, nblk),
        in_specs=[
            pl.BlockSpec((n, LANE), lambda b, j: (row_blk0 + b, j)),
            pl.BlockSpec((8, LANE), lambda b, j: (0, j)),
        ],
        out_specs=pl.BlockSpec((n, LANE), lambda b, j: (b, j)),
        out_shape=jax.ShapeDtypeStruct((BATCH * n, 3 * DN_WIDTH), F32),
        scratch_shapes=[pltpu.VMEM((n + 16, LANE), F32)],
        compiler_params=_cparams(("arbitrary", "arbitrary")),
        name="dn_prep",
    )(main, conv_w)


def _mm(a, b):
    return jnp.dot(a.astype(BF16), b.astype(BF16), preferred_element_type=F32)


def _mm_f32(a, b):
    return jnp.dot(a, b, precision=HI, preferred_element_type=F32)


def _mm_nt(a, b):
    return lax.dot_general(a.astype(BF16), b.astype(BF16), (((1,), (1,)), ((), ())),
                           preferred_element_type=F32)


def _mm_tn(a, b):
    return lax.dot_general(a.astype(BF16), b.astype(BF16), (((0,), (0,)), ((), ())),
                           preferred_element_type=F32)


def _unit_tri_inverse_minus_eye(low):
    n_acc = -low
    p = _mm_f32(low, low)
    steps = int(math.log2(DN_CHUNK)) - 1
    for s in range(steps):
        n_acc = n_acc + p + _mm_f32(n_acc, p)
        if s < steps - 1:
            p = _mm_f32(p, p)
    return n_acc


def _dn_scan_kernel(qf_ref, kf_ref, vf_ref, af_ref, btf_ref, qb_ref, kb_ref, vb_ref, ab_ref, btb_ref,
                    alog_ref, dtb_ref, s0_ref, of_ref, ob_ref, sfin_ref, s_scr):
    t = pl.program_id(1)

    @pl.when(t == 0)
    def _():
        s_scr[...] = s0_ref[...]

    c = DN_CHUNK
    ri = lax.broadcasted_iota(jnp.int32, (c, c), 0)
    ci = lax.broadcasted_iota(jnp.int32, (c, c), 1)
    neg_decay_rate = -jnp.exp(alog_ref[...])
    dtb = dtb_ref[...]

    for d in range(2):
        q_ref, k_ref, v_ref, a_ref, bt_ref, o_ref = (
            (qf_ref, kf_ref, vf_ref, af_ref, btf_ref, of_ref) if d == 0 else
            (qb_ref, kb_ref, vb_ref, ab_ref, btb_ref, ob_ref))
        incl = (ci <= ri) if d == 0 else (ci >= ri)
        strict = (ci < ri) if d == 0 else (ci > ri)
        g = neg_decay_rate * jax.nn.softplus(a_ref[...] + dtb)
        beta = jax.nn.sigmoid(bt_ref[...])
        gcum = jnp.dot(jnp.where(incl, 1.0, 0.0), g, precision=HI, preferred_element_type=F32)
        gcum_t = gcum.T
        for hd in range(DN_HEADS):
            lane = d * DN_HEADS + hd
            sl = slice(hd * DN_HEAD_DIM, (hd + 1) * DN_HEAD_DIM)
            q, k, v = q_ref[:, sl], k_ref[:, sl], v_ref[:, sl]
            gc = gcum[:, lane:lane + 1]
            gr = gcum_t[lane:lane + 1, :]
            bc = beta[:, lane:lane + 1]
            decay = jnp.where(incl, jnp.exp(jnp.where(incl, gc - gr, 0.0)), 0.0)
            k_beta = k * bc
            low = jnp.where(strict, _mm_nt(k_beta, k) * decay, 0.0)
            intra = _mm_nt(q, k) * decay
            inv = _unit_tri_inverse_minus_eye(low)
            eg = jnp.exp(gc)
            rhs_u = v * bc
            rhs_w = k_beta * eg
            u = rhs_u + _mm_f32(inv, rhs_u)
            w = rhs_w + _mm_f32(inv, rhs_w)
            s = s_scr[lane]
            v_new = u - _mm(w, s)
            o_ref[:, sl] = _mm(q * eg, s) + _mm(intra, v_new)
            g_last = gc[c - 1:c, :] if d == 0 else gc[0:1, :]
            s_scr[lane] = s * jnp.exp(g_last) + _mm_tn(k * jnp.exp(g_last - gc), v_new)

    @pl.when(t == pl.num_programs(1) - 1)
    def _():
        sfin_ref[...] = s_scr[...]


def _dn_scan(qkv, main, alog, dtb, s0, n, ab_blk0):
    nc = n // DN_CHUNK
    nh = 2 * DN_HEADS
    fwd = lambda b, t: b * nc + t
    bwd = lambda b, t: b * nc + (nc - 1 - t)
    qspec = lambda order, blk: pl.BlockSpec((DN_CHUNK, DN_WIDTH), lambda b, t: (order(b, t), blk))
    gspec = lambda order, blk: pl.BlockSpec((DN_CHUNK, LANE), lambda b, t: (ab_blk0 + order(b, t), blk))
    vec = pl.BlockSpec((1, LANE), lambda b, t: (0, 0))
    state = pl.BlockSpec((None, nh, DN_HEAD_DIM, DN_HEAD_DIM), lambda b, t: (b, 0, 0, 0))
    return pl.pallas_call(
        _dn_scan_kernel,
        grid=(BATCH, nc),
        in_specs=[
            qspec(fwd, 0), qspec(fwd, 1), qspec(fwd, 2), gspec(fwd, A_BLK128), gspec(fwd, BT_BLK128),
            qspec(bwd, 0), qspec(bwd, 1), qspec(bwd, 2), gspec(bwd, A_BLK128), gspec(bwd, BT_BLK128),
            vec, vec, state,
        ],
        out_specs=[
            pl.BlockSpec((DN_CHUNK, DN_WIDTH), lambda b, t: (fwd(b, t), 0)),
            pl.BlockSpec((DN_CHUNK, DN_WIDTH), lambda b, t: (bwd(b, t), 0)),
            state,
        ],
        out_shape=[jax.ShapeDtypeStruct((BATCH * n, DN_WIDTH), F32),
                   jax.ShapeDtypeStruct((BATCH * n, DN_WIDTH), F32),
                   jax.ShapeDtypeStruct((BATCH, nh, DN_HEAD_DIM, DN_HEAD_DIM), F32)],
        scratch_shapes=[pltpu.VMEM((nh, DN_HEAD_DIM, DN_HEAD_DIM), F32)],
        compiler_params=_cparams(("arbitrary", "arbitrary")),
        name="dn_scan",
    )(qkv, qkv, qkv, main, main, qkv, qkv, qkv, main, main, alog, dtb, s0)


def _rope(x, cos, sin_a, sin_b):
    return x * cos + pltpu.roll(x, LANE - 16, 1) * sin_a + pltpu.roll(x, 16, 1) * sin_b


def _rope_tables():
    rows = SEQ // GRID_W
    r, col = jnp.meshgrid(jnp.arange(rows), jnp.arange(GRID_W), indexing="ij")
    r = r.reshape(-1).astype(F32)
    col = col.reshape(-1).astype(F32)
    axis_dim = DIFF_QK_DIM // 2
    inv_freq = 1.0 / (ROPE_BASE ** (jnp.arange(0, axis_dim, 2, dtype=F32) / axis_dim))
    ar = r[:, None] * inv_freq
    ac = col[:, None] * inv_freq
    ang = jnp.concatenate([ar, ar, ac, ac], axis=-1)
    cos, sin = jnp.cos(ang), jnp.sin(ang)
    first = (np.arange(DIFF_QK_DIM) % 32) < 16
    sin_a = jnp.where(first, -sin, 0.0)
    sin_b = jnp.where(first, 0.0, sin)
    two = lambda a: jnp.concatenate([a, a], axis=-1)
    return two(cos), two(sin_a), two(sin_b)


def _rope_prep_kernel(q_ref, k_ref, v_ref, cos_ref, sa_ref, sb_ref, qo_ref, ko_ref, vo_ref):
    cos, sa, sb = cos_ref[...], sa_ref[...], sb_ref[...]
    for hd in range(DIFF_HEADS):
        sl = slice(hd * LANE, (hd + 1) * LANE)
        qo_ref[:, sl] = (_rope(q_ref[:, sl], cos, sa, sb) * DIFF_QK_DIM ** -0.5).astype(BF16)
        ko_ref[:, sl] = _rope(k_ref[:, sl], cos, sa, sb).astype(BF16)
    vo_ref[...] = v_ref[...].astype(BF16)


def _rope_prep(main, tables):
    tr = 512
    per = SEQ // tr
    blk = lambda c: pl.BlockSpec((tr, DIFF_WIDTH), lambda i: (i, c))
    tab = pl.BlockSpec((tr, LANE), lambda i: (i % per, 0))
    out = pl.BlockSpec((tr, DIFF_WIDTH), lambda i: (i, 0))
    shape = jax.ShapeDtypeStruct((LAT_ROWS, DIFF_WIDTH), BF16)
    return pl.pallas_call(
        _rope_prep_kernel,
        grid=(LAT_ROWS // tr,),
        in_specs=[blk(3), blk(4), blk(5), tab, tab, tab],
        out_specs=[out, out, out],
        out_shape=[shape, shape, shape],
        compiler_params=_cparams(("arbitrary",)),
        name="rope_prep",
    )(main, main, main, *tables)


def _lam(lam_ref, lam_init):
    x = lam_ref[...]
    s1 = jnp.sum(x[0:1] * x[1:2], axis=-1, keepdims=True)
    s2 = jnp.sum(x[2:3] * x[3:4], axis=-1, keepdims=True)
    return jnp.exp(s1) - jnp.exp(s2) + lam_init


def _two_maps(q):
    lane = lax.broadcasted_iota(jnp.int32, q.shape, 1)
    zero = jnp.zeros_like(q)
    return jnp.where(lane < DIFF_QK_DIM, q, zero), jnp.where(lane >= DIFF_QK_DIM, q, zero)


def _softmax_parts(scores):
    m = functools.reduce(jnp.maximum, [jnp.max(s, axis=-1, keepdims=True) for s in scores])
    e = [jnp.exp(s - m) for s in scores]
    r = 1.0 / functools.reduce(lambda a, b: a + b, [jnp.sum(x, axis=-1, keepdims=True) for x in e])
    return [x * r for x in e]


def _diff_attend(q, keys, values, lam):
    q1, q2 = _two_maps(q)
    p1 = _softmax_parts([_mm_nt(q1, k) for k in keys])
    p2 = _softmax_parts([_mm_nt(q2, k) for k in keys])
    out = None
    for a, b, v in zip(p1, p2, values):
        o = _mm(a - lam * b, v)
        out = o if out is None else out + o
    return out


def _attn_lat_kernel(q_ref, kl_ref, vl_ref, kc_ref, vc_ref, lam_ref, gain_ref, out_ref, *, lam_init):
    lam = _lam(lam_ref, lam_init)
    o = _diff_attend(q_ref[...], [kc_ref[...].astype(BF16), kl_ref[...]],
                     [vc_ref[...].astype(BF16), vl_ref[...]], lam)
    out_ref[...] = (_rms(o) * gain_ref[...] * (1.0 - lam_init)).astype(BF16)


def _attn_ctx_kernel(q_ref, kc_ref, vc_ref, lam_ref, gain_ref, out_ref, *, lam_init):
    lam = _lam(lam_ref, lam_init)
    q = (q_ref[...] * DIFF_QK_DIM ** -0.5).astype(BF16)
    o = _diff_attend(q, [kc_ref[...].astype(BF16)], [vc_ref[...].astype(BF16)], lam)
    out_ref[...] = (_rms(o) * gain_ref[...] * (1.0 - lam_init)).astype(BF16)


def _attn_lat(qr, kr, vr, main, lam_p, gain, lam_init):
    tq = ATT_Q_TILE
    nq = SEQ // tq
    ctx_blk = LAT_ROWS // CTX_LEN
    return pl.pallas_call(
        functools.partial(_attn_lat_kernel, lam_init=lam_init),
        grid=(BATCH, DIFF_HEADS, nq),
        in_specs=[
            pl.BlockSpec((tq, LANE), lambda b, h, i: (b * nq + i, h)),
            pl.BlockSpec((SEQ, LANE), lambda b, h, i: (b, h)),
            pl.BlockSpec((SEQ, LANE), lambda b, h, i: (b, h)),
            pl.BlockSpec((CTX_LEN, LANE), lambda b, h, i: (ctx_blk + b, CK_BLK128 + h)),
            pl.BlockSpec((CTX_LEN, LANE), lambda b, h, i: (ctx_blk + b, CV_BLK128 + h)),
            pl.BlockSpec((4, DIFF_QK_DIM), lambda b, h, i: (0, 0)),
            pl.BlockSpec((1, LANE), lambda b, h, i: (0, 0)),
        ],
        out_specs=pl.BlockSpec((tq, LANE), lambda b, h, i: (b * nq + i, h)),
        out_shape=jax.ShapeDtypeStruct((LAT_ROWS, DIFF_WIDTH), BF16),
        compiler_params=_cparams(("arbitrary", "arbitrary", "arbitrary")),
        name="attn_lat",
    )(qr, kr, vr, main, main, lam_p, gain)


def _attn_ctx(main, lam_p, gain, lam_init):
    ctx_blk = LAT_ROWS // CTX_LEN
    blk = lambda c0: pl.BlockSpec((CTX_LEN, LANE), lambda b, h: (ctx_blk + b, c0 + h))
    return pl.pallas_call(
        functools.partial(_attn_ctx_kernel, lam_init=lam_init),
        grid=(BATCH, DIFF_HEADS),
        in_specs=[
            blk(CQ_BLK128), blk(CK_BLK128), blk(CV_BLK128),
            pl.BlockSpec((4, DIFF_QK_DIM), lambda b, h: (0, 0)),
            pl.BlockSpec((1, LANE), lambda b, h: (0, 0)),
        ],
        out_specs=pl.BlockSpec((CTX_LEN, LANE), lambda b, h: (b, h)),
        out_shape=jax.ShapeDtypeStruct((CTX_ROWS, DIFF_WIDTH), BF16),
        compiler_params=_cparams(("arbitrary", "arbitrary")),
        name="attn_ctx",
    )(main, main, main, lam_p, gain)


def _mixout_kernel(h_ref, gate_ref, gpost_ref, f_ref, of_ref, ob_ref, dg_ref, att_ref, dnn_ref, wo_ref,
                   out_ref, mix_scr):
    mix_scr[:, 0:FOURIER_WIDTH] = f_ref[...].astype(BF16)
    for hd in range(DN_HEADS):
        sl = slice(hd * LANE, (hd + 1) * LANE)
        o = _rms(of_ref[:, sl] + ob_ref[:, sl]) * dnn_ref[...]
        dst = slice(FOURIER_WIDTH + hd * LANE, FOURIER_WIDTH + (hd + 1) * LANE)
        mix_scr[:, dst] = (o * _silu(dg_ref[:, sl])).astype(BF16)
    mix_scr[:, FOURIER_WIDTH + DN_WIDTH:] = att_ref[...]
    m = jnp.dot(mix_scr[...], wo_ref[...], preferred_element_type=F32)
    out_ref[...] = h_ref[...] + gate_ref[...] * (_rms(m) * gpost_ref[...])


def _mixout(h, mod, gpost, f_out, o_f, o_b, main, att, dn_norm, wo):
    tm = ROW_TILE
    row = lambda w: pl.BlockSpec((tm, w), lambda i: (i, 0))
    return pl.pallas_call(
        _mixout_kernel,
        grid=(ROWS // tm,),
        in_specs=[
            row(D_MODEL),
            pl.BlockSpec((None, None, 1, D_MODEL), lambda i: (_mod_row(i, tm), 5, 0, 0)),
            pl.BlockSpec((1, D_MODEL), lambda i: (0, 0)),
            row(FOURIER_WIDTH), row(DN_WIDTH), row(DN_WIDTH),
            pl.BlockSpec((tm, DN_WIDTH), lambda i: (i, DG_BLK768)),
            row(DIFF_WIDTH),
            pl.BlockSpec((1, LANE), lambda i: (0, 0)),
            pl.BlockSpec((D_MODEL, D_MODEL), lambda i: (0, 0)),
        ],
        out_specs=row(D_MODEL),
        out_shape=jax.ShapeDtypeStruct((ROWS, D_MODEL), F32),
        scratch_shapes=[pltpu.VMEM((tm, D_MODEL), BF16)],
        compiler_params=_cparams(("arbitrary",)),
        name="mixout",
    )(h, mod, gpost, f_out, o_f, o_b, main, att, dn_norm, wo)


def _pack_ffn_in(w):
    lead = w.shape[:-1]
    pad = [(0, 0)] * len(lead) + [(0, FF_PAD - D_FF)]
    gate = jnp.pad(w[..., :D_FF], pad).reshape(*lead, FF_PAD // FF_TILE, 1, FF_TILE)
    up = jnp.pad(w[..., D_FF:], pad).reshape(*lead, FF_PAD // FF_TILE, 1, FF_TILE)
    return jnp.concatenate([gate, up], axis=-2).reshape(*lead, 2 * FF_PAD).astype(BF16)


def _pack_w_in(w):
    o = 0
    seg = {}
    for name, size in (("f", 512), ("qkv", 2304), ("dg", 768), ("a", 12), ("bt", 12),
                       ("cq", 768), ("ck", 768), ("cv", 768)):
        seg[name] = w[..., o:o + size]
        o += size
    lane_pad = lambda a: jnp.pad(a, [(0, 0), (0, 0), (0, LANE - a.shape[-1])])
    return jnp.concatenate([seg["qkv"], seg["cq"], seg["ck"], seg["cv"], seg["dg"],
                            lane_pad(seg["a"]), lane_pad(seg["bt"]), seg["f"]], axis=-1).astype(BF16)


def kernel(x, c, ctx, c_ctx, w_ada, b_ada, norm_pre, norm_post, ffn_w_in, ffn_w_out, w_in, dn_conv,
           dn_a_log, dn_dt_bias, dn_norm, diff_lambda, diff_norm, w_out):
    h = jnp.concatenate([x.reshape(LAT_ROWS, D_MODEL), ctx.reshape(CTX_ROWS, D_MODEL)], axis=0)

    c8 = jnp.concatenate([c, c_ctx[None, :], jnp.zeros((3, D_MODEL), F32)], axis=0)
    mod_all = _ada(c8, w_ada, b_ada).reshape(DEPTH, 8, N_MOD, 1, D_MODEL)

    wgu_all = _pack_ffn_in(ffn_w_in)
    wo_all = jnp.pad(ffn_w_out, [(0, 0), (0, 0), (0, FF_PAD - D_FF), (0, 0)]).astype(BF16)
    w_in_all = _pack_w_in(w_in)
    w_out_all = w_out.astype(BF16)
    conv_all = jnp.pad(dn_conv, [(0, 0), (0, 8 - DN_CONV), (0, 0)])
    lane_vec = lambda a: jnp.pad(a.reshape(DEPTH, 1, 2 * DN_HEADS), [(0, 0), (0, 0), (0, LANE - 2 * DN_HEADS)])
    alog_all = lane_vec(dn_a_log)
    dtb_all = lane_vec(dn_dt_bias)

    fc = _fourier_consts()
    tables = _rope_tables()
    s_zero = jnp.zeros((BATCH, 2 * DN_HEADS, DN_HEAD_DIM, DN_HEAD_DIM), F32)

    for layer in range(DEPTH):
        lam_init = 0.8 - 0.6 * math.exp(-0.3 * layer)
        mod = mod_all[layer]
        gpre = lambda s: norm_pre[layer, s].reshape(1, D_MODEL)
        gpost = lambda s: norm_post[layer, s].reshape(1, D_MODEL)

        h = _ffn(h, mod, gpre(0), gpost(0), wgu_all[layer, 0], wo_all[layer, 0], 0)

        main, f = _proj(h, mod, gpre(1), w_in_all[layer])
        f_out = jnp.concatenate([_fourier_lat(f, fc), _fourier_ctx(f, fc)], axis=0)
        qkv_c = _dn_prep(main, conv_all[layer], CTX_LEN, LAT_ROWS // CTX_LEN)
        qkv_l = _dn_prep(main, conv_all[layer], SEQ, 0)
        of_c, ob_c, s_ctx = _dn_scan(qkv_c, main, alog_all[layer], dtb_all[layer], s_zero, CTX_LEN,
                                     LAT_ROWS // DN_CHUNK)
        of_l, ob_l, _ = _dn_scan(qkv_l, main, alog_all[layer], dtb_all[layer], s_ctx, SEQ, 0)
        o_f = jnp.concatenate([of_l, of_c], axis=0)
        o_b = jnp.concatenate([ob_l, ob_c], axis=0)
        qr, kr, vr = _rope_prep(main, tables)
        gain = diff_norm[layer].reshape(1, DIFF_V_DIM)
        att = jnp.concatenate([_attn_lat(qr, kr, vr, main, diff_lambda[layer], gain, lam_init),
                               _attn_ctx(main, diff_lambda[layer], gain, lam_init)], axis=0)

        h = _mixout(h, mod, gpost(1), f_out, o_f, o_b, main, att,
                    dn_norm[layer].reshape(1, DN_HEAD_DIM), w_out_all[layer])
        h = _ffn(h, mod, gpre(2), gpost(2), wgu_all[layer, 1], wo_all[layer, 1], 2)

    return h[:LAT_ROWS].reshape(BATCH, SEQ, D_MODEL)
```

```python
import functools
import math

import numpy as np
import jax
import jax.numpy as jnp
from jax import lax
from jax.experimental import pallas as pl
from jax.experimental.pallas import tpu as pltpu

F32 = jnp.float32
BF16 = jnp.bfloat16

D_MODEL = 2048
BATCH = 4
SEQ = 4096
DEPTH = 4
GRID_W = 64
CTX_LEN = 256
N_MOD = 9
D_FF = 5504
EPS = 1e-6
FOURIER_WIDTH = 512
FOURIER_GROUP_DIM = 128
DN_HEAD_DIM = 128
DN_HEADS = 6
DN_WIDTH = 768
DN_CHUNK = 64
DN_CONV = 5
DIFF_V_DIM = 128
DIFF_QK_DIM = 64
DIFF_HEADS = 6
DIFF_WIDTH = 768
ROPE_BASE = 10000.0

LAT_ROWS = BATCH * SEQ
CTX_ROWS = BATCH * CTX_LEN
ROWS = LAT_ROWS + CTX_ROWS

LANE = 128
ROW_TILE = 512
FF_TILE = 512
FF_PAD = 5632
MAIN_WIDTH = 5632
PROJ_TILE = 512
QKV_BLK768 = 0
CQ_BLK128 = 18
CK_BLK128 = 24
CV_BLK128 = 30
DG_BLK768 = 6
A_BLK128 = 42
BT_BLK128 = 43
ATT_Q_TILE = 256
FFT_R = 64
FFT_G = 8
VMEM_LIMIT = 56 * 1024 * 1024

HI = lax.Precision.HIGHEST


def _cparams(sem):
    return pltpu.CompilerParams(dimension_semantics=sem, vmem_limit_bytes=VMEM_LIMIT)


def _rms(x):
    return x * lax.rsqrt(jnp.mean(x * x, axis=-1, keepdims=True) + EPS)


def _silu(x):
    return x * jax.nn.sigmoid(x)


def _mod_row(i, tile):
    return jnp.where(i < LAT_ROWS // tile, i // (SEQ // tile), BATCH)


def _mod_spec(k, tile):
    return pl.BlockSpec((None, None, 1, D_MODEL), lambda i, j, k=k: (_mod_row(i, tile), k, 0, 0))


def _ada_kernel(c_ref, w_ref, b_ref, out_ref):
    x = _silu(c_ref[...]).astype(BF16)
    out_ref[...] = jnp.dot(x, w_ref[...].astype(BF16), preferred_element_type=F32) + b_ref[...]


def _ada(c8, w_ada, b_ada):
    tn = 1024
    nt = N_MOD * D_MODEL // tn
    return pl.pallas_call(
        _ada_kernel,
        grid=(DEPTH, nt),
        in_specs=[
            pl.BlockSpec((8, D_MODEL), lambda l, j: (0, 0)),
            pl.BlockSpec((None, D_MODEL, tn), lambda l, j: (l, 0, j)),
            pl.BlockSpec((None, 1, tn), lambda l, j: (l, 0, j)),
        ],
        out_specs=pl.BlockSpec((None, 8, tn), lambda l, j: (l, 0, j)),
        out_shape=jax.ShapeDtypeStruct((DEPTH, 8, N_MOD * D_MODEL), F32),
        compiler_params=_cparams(("arbitrary", "arbitrary")),
        name="ada",
    )(c8, w_ada, b_ada.reshape(DEPTH, 1, N_MOD * D_MODEL))


def _ffn_kernel(h_ref, shift_ref, scale_ref, gate_ref, gpre_ref, gpost_ref, wg_ref, wu_ref, wo_ref,
                out_ref, y_scr, acc_scr):
    j = pl.program_id(1)

    @pl.when(j == 0)
    def _():
        y = _rms(h_ref[...]) * gpre_ref[...] * (1.0 + scale_ref[...]) + shift_ref[...]
        y_scr[...] = y.astype(BF16)
        acc_scr[...] = jnp.zeros_like(acc_scr)

    y = y_scr[...]
    g = jnp.dot(y, wg_ref[...], preferred_element_type=F32)
    a = _silu(g) * jnp.dot(y, wu_ref[...], preferred_element_type=F32)
    acc_scr[...] += jnp.dot(a.astype(BF16), wo_ref[...], preferred_element_type=F32)

    @pl.when(j == pl.num_programs(1) - 1)
    def _():
        out_ref[...] = h_ref[...] + 0.5 * gate_ref[...] * (_rms(acc_scr[...]) * gpost_ref[...])


def _ffn(h, mod, gpre, gpost, wg, wu, wo, sub):
    tm = ROW_TILE
    row = pl.BlockSpec((tm, D_MODEL), lambda i, j: (i, 0))
    vec = pl.BlockSpec((1, D_MODEL), lambda i, j: (0, 0))
    wcol = pl.BlockSpec((D_MODEL, FF_TILE), lambda i, j: (0, j))
    return pl.pallas_call(
        _ffn_kernel,
        grid=(ROWS // tm, FF_PAD // FF_TILE),
        in_specs=[
            row, _mod_spec(3 * sub, tm), _mod_spec(3 * sub + 1, tm), _mod_spec(3 * sub + 2, tm), vec, vec,
            wcol, wcol,
            pl.BlockSpec((FF_TILE, D_MODEL), lambda i, j: (j, 0)),
        ],
        out_specs=row,
        out_shape=jax.ShapeDtypeStruct((ROWS, D_MODEL), F32),
        scratch_shapes=[pltpu.VMEM((tm, D_MODEL), BF16), pltpu.VMEM((tm, D_MODEL), F32)],
        compiler_params=_cparams(("arbitrary", "arbitrary")),
        name="ffn",
    )(h, mod, mod, mod, gpre, gpost, wg, wu, wo)


def _proj_kernel(h_ref, shift_ref, scale_ref, gpre_ref, w_ref, main_ref, f_ref, y_scr):
    j = pl.program_id(1)

    @pl.when(j == 0)
    def _():
        y = _rms(h_ref[...]) * gpre_ref[...] * (1.0 + scale_ref[...]) + shift_ref[...]
        y_scr[...] = y.astype(BF16)

    res = jnp.dot(y_scr[...], w_ref[...], preferred_element_type=F32)
    n_main = MAIN_WIDTH // PROJ_TILE

    @pl.when(j < n_main)
    def _():
        main_ref[...] = res

    @pl.when(j == n_main)
    def _():
        f_ref[...] = res


def _proj(h, mod, gpre, w):
    tm = ROW_TILE
    n_main = MAIN_WIDTH // PROJ_TILE
    return pl.pallas_call(
        _proj_kernel,
        grid=(ROWS // tm, n_main + 1),
        in_specs=[
            pl.BlockSpec((tm, D_MODEL), lambda i, j: (i, 0)),
            _mod_spec(3, tm), _mod_spec(4, tm),
            pl.BlockSpec((1, D_MODEL), lambda i, j: (0, 0)),
            pl.BlockSpec((D_MODEL, PROJ_TILE), lambda i, j: (0, j)),
        ],
        out_specs=[
            pl.BlockSpec((tm, PROJ_TILE), lambda i, j: (i, jnp.minimum(j, n_main - 1))),
            pl.BlockSpec((tm, FOURIER_WIDTH), lambda i, j: (i, 0)),
        ],
        out_shape=[jax.ShapeDtypeStruct((ROWS, MAIN_WIDTH), F32),
                   jax.ShapeDtypeStruct((ROWS, FOURIER_WIDTH), F32)],
        scratch_shapes=[pltpu.VMEM((tm, D_MODEL), BF16)],
        compiler_params=_cparams(("arbitrary", "arbitrary")),
        name="proj",
    )(h, mod, mod, gpre, w)


def _dft_mats(n):
    k = np.arange(n)
    ang = 2.0 * np.pi * ((k[:, None] * k[None, :]) % n) / n
    return np.cos(ang), np.sin(ang)


def _fourier_consts():
    c64, s64 = _dft_mats(FFT_R)
    stage_a = np.concatenate([c64, -s64], axis=0)
    stage_b = np.block([[c64, s64], [-s64, c64]])
    k2 = np.arange(FFT_R)[:, None]
    n1 = np.arange(FFT_R)[None, :]
    ang = 2.0 * np.pi * (k2 * n1) / (FFT_R * FFT_R)
    tw_c, tw_s = np.cos(ang), np.sin(ang)
    cg, sg = _dft_mats(FOURIER_GROUP_DIM)
    eye = np.eye(FOURIER_WIDTH // FOURIER_GROUP_DIM)
    cbd, sbd = np.kron(eye, cg), np.kron(eye, sg)
    f32 = lambda a: jnp.asarray(a, F32)
    return dict(stage_a=f32(stage_a), stage_b=f32(stage_b), tw_c=f32(tw_c), tw_s=f32(tw_s),
                cbd=cbd, sbd=sbd)


def _fft_a_kernel(x_ref, fa_ref, twc_ref, tws_ref, out_ref):
    a = jnp.dot(fa_ref[...], x_ref[...], precision=HI, preferred_element_type=F32)
    ar, ai = a[:FFT_R], a[FFT_R:]
    tc, ts = twc_ref[...], tws_ref[...]
    br = ar * tc + ai * ts
    bi = ai * tc - ar * ts
    for g in range(FFT_G):
        sl = slice(g * FOURIER_WIDTH, (g + 1) * FOURIER_WIDTH)
        out_ref[0, g] = br[:, sl]
        out_ref[1, g] = bi[:, sl]


def _fft_b_kernel(x_ref, fb_ref, cs_ref, out_ref):
    y = jnp.dot(fb_ref[...], x_ref[...], precision=HI, preferred_element_type=F32)
    yr, yi = y[:FFT_R], y[FFT_R:]
    for g in range(FFT_G):
        sl = slice(g * FOURIER_WIDTH, (g + 1) * FOURIER_WIDTH)
        yy = jnp.concatenate([yr[:, sl], yi[:, sl]], axis=1)
        out_ref[:, sl] = jnp.dot(yy, cs_ref[...], precision=HI, preferred_element_type=F32)


def _fourier_lat(f, fc):
    wide = FFT_R * FOURIER_WIDTH
    gw = FFT_G * FOURIER_WIDTH
    x2 = f.reshape(ROWS // FFT_R, wide)
    scale = 1.0 / math.sqrt(SEQ * FOURIER_GROUP_DIM)
    twc = jnp.repeat(fc["tw_c"], FOURIER_WIDTH, axis=1)
    tws = jnp.repeat(fc["tw_s"], FOURIER_WIDTH, axis=1)
    cs = jnp.asarray(np.concatenate([fc["cbd"], fc["sbd"]], axis=0) * scale, F32)
    mid = pl.pallas_call(
        _fft_a_kernel,
        grid=(BATCH, FFT_R // FFT_G),
        in_specs=[
            pl.BlockSpec((FFT_R, gw), lambda b, i: (b, i)),
            pl.BlockSpec((2 * FFT_R, FFT_R), lambda b, i: (0, 0)),
            pl.BlockSpec((FFT_R, gw), lambda b, i: (0, i)),
            pl.BlockSpec((FFT_R, gw), lambda b, i: (0, i)),
        ],
        out_specs=pl.BlockSpec((None, 2, FFT_G, FFT_R, FOURIER_WIDTH), lambda b, i: (b, 0, i, 0, 0)),
        out_shape=jax.ShapeDtypeStruct((BATCH, 2, FFT_R, FFT_R, FOURIER_WIDTH), F32),
        compiler_params=_cparams(("arbitrary", "arbitrary")),
        name="fft_a",
    )(x2, fc["stage_a"], twc, tws)
    mid = mid.reshape(BATCH, 2 * FFT_R, wide)
    out = pl.pallas_call(
        _fft_b_kernel,
        grid=(BATCH, FFT_R // FFT_G),
        in_specs=[
            pl.BlockSpec((None, 2 * FFT_R, gw), lambda b, i: (b, 0, i)),
            pl.BlockSpec((2 * FFT_R, 2 * FFT_R), lambda b, i: (0, 0)),
            pl.BlockSpec((2 * FOURIER_WIDTH, FOURIER_WIDTH), lambda b, i: (0, 0)),
        ],
        out_specs=pl.BlockSpec((FFT_R, gw), lambda b, i: (b, i)),
        out_shape=jax.ShapeDtypeStruct((LAT_ROWS // FFT_R, wide), F32),
        compiler_params=_cparams(("arbitrary", "arbitrary")),
        name="fft_b",
    )(mid, fc["stage_b"], cs)
    return out.reshape(LAT_ROWS, FOURIER_WIDTH)


def _fourier_ctx_kernel(x_ref, cn_ref, sn_ref, cbd_ref, sbd_ref, out_ref):
    x = x_ref[...]
    p = jnp.dot(x, cbd_ref[...], precision=HI, preferred_element_type=F32)
    q = jnp.dot(x, sbd_ref[...], precision=HI, preferred_element_type=F32)
    out_ref[...] = (jnp.dot(cn_ref[...], p, precision=HI, preferred_element_type=F32)
                    - jnp.dot(sn_ref[...], q, precision=HI, preferred_element_type=F32))


def _fourier_ctx(f, fc):
    cn, sn = _dft_mats(CTX_LEN)
    scale = 1.0 / math.sqrt(CTX_LEN * FOURIER_GROUP_DIM)
    const = lambda shape: pl.BlockSpec(shape, lambda b: (0, 0))
    return pl.pallas_call(
        _fourier_ctx_kernel,
        grid=(BATCH,),
        in_specs=[
            pl.BlockSpec((CTX_LEN, FOURIER_WIDTH), lambda b: (LAT_ROWS // CTX_LEN + b, 0)),
            const((CTX_LEN, CTX_LEN)), const((CTX_LEN, CTX_LEN)),
            const((FOURIER_WIDTH, FOURIER_WIDTH)), const((FOURIER_WIDTH, FOURIER_WIDTH)),
        ],
        out_specs=pl.BlockSpec((CTX_LEN, FOURIER_WIDTH), lambda b: (b, 0)),
        out_shape=jax.ShapeDtypeStruct((CTX_ROWS, FOURIER_WIDTH), F32),
        compiler_params=_cparams(("arbitrary",)),
        name="fourier_ctx",
    )(f, jnp.asarray(cn, F32), jnp.asarray(sn, F32),
      jnp.asarray(fc["cbd"] * scale, F32), jnp.asarray(fc["sbd"] * scale, F32))


def _dn_prep_kernel(x_ref, w_ref, out_ref, pad_scr, *, n):
    j = pl.program_id(1)
    half = DN_CONV // 2
    pad_scr[0:8, :] = jnp.zeros((8, LANE), F32)
    pad_scr[8 + n:16 + n, :] = jnp.zeros((8, LANE), F32)
    pad_scr[8:8 + n, :] = x_ref[...]
    w = w_ref[...]
    is_qk = j < 2 * DN_HEADS
    post = jnp.where(j < DN_HEADS, DN_HEAD_DIM ** -0.5, 1.0).astype(F32)
    rc = 256
    for c in range(n // rc):
        acc = jnp.zeros((rc, LANE), F32)
        for t in range(DN_CONV):
            acc = acc + pad_scr[pl.ds(8 + c * rc + t - half, rc), :] * w[t:t + 1, :]
        y = _silu(acc)
        nrm = y * lax.rsqrt(jnp.sum(y * y, axis=-1, keepdims=True) + EPS) * post
        out_ref[c * rc:(c + 1) * rc, :] = jnp.where(is_qk, nrm, y)


def _dn_prep(main, conv_w, n, row_blk0):
    nblk = 3 * DN_WIDTH // LANE
    return pl.pallas_call(
        functools.partial(_dn_prep_kernel, n=n),
        grid=(BATCH, nblk),
        in_specs=[
            pl.BlockSpec((n, LANE), lambda b, j: (row_blk0 + b, j)),
            pl.BlockSpec((8, LANE), lambda b, j: (0, j)),
        ],
        out_specs=pl.BlockSpec((n, LANE), lambda b, j: (b, j)),
        out_shape=jax.ShapeDtypeStruct((BATCH * n, 3 * DN_WIDTH), F32),
        scratch_shapes=[pltpu.VMEM((n + 16, LANE), F32)],
        compiler_params=_cparams(("arbitrary", "arbitrary")),
        name="dn_prep",
    )(main, conv_w)


def _mm(a, b):
    return jnp.dot(a.astype(BF16), b.astype(BF16), preferred_element_type=F32)


def _mm_f32(a, b):
    return jnp.dot(a, b, precision=HI, preferred_element_type=F32)


def _mm_nt(a, b):
    return lax.dot_general(a.astype(BF16), b.astype(BF16), (((1,), (1,)), ((), ())),
                           preferred_element_type=F32)


def _mm_tn(a, b):
    return lax.dot_general(a.astype(BF16), b.astype(BF16), (((0,), (0,)), ((), ())),
                           preferred_element_type=F32)


def _unit_tri_inverse_minus_eye(low):
    n_acc = -low
    p = _mm_f32(low, low)
    steps = int(math.log2(DN_CHUNK)) - 1
    for s in range(steps):
        n_acc = n_acc + p + _mm_f32(n_acc, p)
        if s < steps - 1:
            p = _mm_f32(p, p)
    return n_acc


def _dn_scan_kernel(qf_ref, kf_ref, vf_ref, af_ref, btf_ref, qb_ref, kb_ref, vb_ref, ab_ref, btb_ref,
                    alog_ref, dtb_ref, s0_ref, of_ref, ob_ref, sfin_ref, s_scr):
    t = pl.program_id(1)

    @pl.when(t == 0)
    def _():
        s_scr[...] = s0_ref[...]

    c = DN_CHUNK
    ri = lax.broadcasted_iota(jnp.int32, (c, c), 0)
    ci = lax.broadcasted_iota(jnp.int32, (c, c), 1)
    neg_decay_rate = -jnp.exp(alog_ref[...])
    dtb = dtb_ref[...]

    for d in range(2):
        q_ref, k_ref, v_ref, a_ref, bt_ref, o_ref = (
            (qf_ref, kf_ref, vf_ref, af_ref, btf_ref, of_ref) if d == 0 else
            (qb_ref, kb_ref, vb_ref, ab_ref, btb_ref, ob_ref))
        incl = (ci <= ri) if d == 0 else (ci >= ri)
        strict = (ci < ri) if d == 0 else (ci > ri)
        g = neg_decay_rate * jax.nn.softplus(a_ref[...] + dtb)
        beta = jax.nn.sigmoid(bt_ref[...])
        gcum = jnp.dot(jnp.where(incl, 1.0, 0.0), g, precision=HI, preferred_element_type=F32)
        gcum_t = gcum.T
        for hd in range(DN_HEADS):
            lane = d * DN_HEADS + hd
            sl = slice(hd * DN_HEAD_DIM, (hd + 1) * DN_HEAD_DIM)
            q, k, v = q_ref[:, sl], k_ref[:, sl], v_ref[:, sl]
            gc = gcum[:, lane:lane + 1]
            gr = gcum_t[lane:lane + 1, :]
            bc = beta[:, lane:lane + 1]
            decay = jnp.where(incl, jnp.exp(jnp.where(incl, gc - gr, 0.0)), 0.0)
            k_beta = k * bc
            low = jnp.where(strict, _mm_nt(k_beta, k) * decay, 0.0)
            intra = _mm_nt(q, k) * decay
            inv = _unit_tri_inverse_minus_eye(low)
            eg = jnp.exp(gc)
            rhs =jnp.concatenate([v * bc, k_beta * eg], axis=1)
            sol = rhs + _mm_f32(inv, rhs)
            u, w = sol[:, :DN_HEAD_DIM], sol[:, DN_HEAD_DIM:]
            s = s_scr[lane]
            v_new = u - _mm(w, s)
            o_ref[:, sl] = _mm(q * eg, s) + _mm(intra, v_new)
            g_last = gc[c - 1:c, :] if d == 0 else gc[0:1, :]
            s_scr[lane] = s * jnp.exp(g_last) + _mm_tn(k * jnp.exp(g_last - gc), v_new)

    @pl.when(t == pl.num_programs(1) - 1)
    def _():
        sfin_ref[...] = s_scr[...]


def _dn_scan(qkv, main, alog, dtb, s0, n, ab_blk0):
    nc = n // DN_CHUNK
    nh = 2 * DN_HEADS
    fwd = lambda b, t: b * nc + t
    bwd = lambda b, t: b * nc + (nc - 1 - t)
    qspec = lambda order, blk: pl.BlockSpec((DN_CHUNK, DN_WIDTH), lambda b, t: (order(b, t), blk))
    gspec = lambda order, blk: pl.BlockSpec((DN_CHUNK, LANE), lambda b, t: (ab_blk0 + order(b, t), blk))
    vec = pl.BlockSpec((1, LANE), lambda b, t: (0, 0))
    state = pl.BlockSpec((None, nh, DN_HEAD_DIM, DN_HEAD_DIM), lambda b, t: (b, 0, 0, 0))
    return pl.pallas_call(
        _dn_scan_kernel,
        grid=(BATCH, nc),
        in_specs=[
            qspec(fwd, 0), qspec(fwd, 1), qspec(fwd, 2), gspec(fwd, A_BLK128), gspec(fwd, BT_BLK128),
            qspec(bwd, 0), qspec(bwd, 1), qspec(bwd, 2), gspec(bwd, A_BLK128), gspec(bwd, BT_BLK128),
            vec, vec, state,
        ],
        out_specs=[
            pl.BlockSpec((DN_CHUNK, DN_WIDTH), lambda b, t: (fwd(b, t), 0)),
            pl.BlockSpec((DN_CHUNK, DN_WIDTH), lambda b, t: (bwd(b, t), 0)),
            state,
        ],
        out_shape=[jax.ShapeDtypeStruct((BATCH * n, DN_WIDTH), F32),
                   jax.ShapeDtypeStruct((BATCH * n, DN_WIDTH), F32),
                   jax.ShapeDtypeStruct((BATCH, nh, DN_HEAD_DIM, DN_HEAD_DIM), F32)],
        scratch_shapes=[pltpu.VMEM((nh, DN_HEAD_DIM, DN_HEAD_DIM), F32)],
        compiler_params=_cparams(("arbitrary", "arbitrary")),
        name="dn_scan",
    )(qkv, qkv, qkv, main, main, qkv, qkv, qkv, main, main, alog, dtb, s0)


def _rope(x, cos, sin_a, sin_b):
    return x * cos + pltpu.roll(x, LANE - 16, 1) * sin_a + pltpu.roll(x, 16, 1) * sin_b


def _rope_tables():
    rows = SEQ // GRID_W
    r, col = jnp.meshgrid(jnp.arange(rows), jnp.arange(GRID_W), indexing="ij")
    r = r.reshape(-1).astype(F32)
    col = col.reshape(-1).astype(F32)
    axis_dim = DIFF_QK_DIM // 2
    inv_freq = 1.0 / (ROPE_BASE ** (jnp.arange(0, axis_dim, 2, dtype=F32) / axis_dim))
    ar = r[:, None] * inv_freq
    ac = col[:, None] * inv_freq
    ang = jnp.concatenate([ar, ar, ac, ac], axis=-1)
    cos, sin = jnp.cos(ang), jnp.sin(ang)
    first = (np.arange(DIFF_QK_DIM) % 32) < 16
    sin_a = jnp.where(first, -sin, 0.0)
    sin_b = jnp.where(first, 0.0, sin)
    two = lambda a: jnp.concatenate([a, a], axis=-1)
    return two(cos), two(sin_a), two(sin_b)


def _rope_prep_kernel(q_ref, k_ref, v_ref, cos_ref, sa_ref, sb_ref, qo_ref, ko_ref, vo_ref):
    cos, sa, sb = cos_ref[...], sa_ref[...], sb_ref[...]
    for hd in range(DIFF_HEADS):
        sl = slice(hd * LANE, (hd + 1) * LANE)
        qo_ref[:, sl] = (_rope(q_ref[:, sl], cos, sa, sb) * DIFF_QK_DIM ** -0.5).astype(BF16)
        ko_ref[:, sl] = _rope(k_ref[:, sl], cos, sa, sb).astype(BF16)
    vo_ref[...] = v_ref[...].astype(BF16)


def _rope_prep(main, tables):
    tr = 512
    per = SEQ // tr
    blk = lambda c: pl.BlockSpec((tr, DIFF_WIDTH), lambda i: (i, c))
    tab = pl.BlockSpec((tr, LANE), lambda i: (i % per, 0))
    out = pl.BlockSpec((tr, DIFF_WIDTH), lambda i: (i, 0))
    shape = jax.ShapeDtypeStruct((LAT_ROWS, DIFF_WIDTH), BF16)
    return pl.pallas_call(
        _rope_prep_kernel,
        grid=(LAT_ROWS // tr,),
        in_specs=[blk(3), blk(4), blk(5), tab, tab, tab],
        out_specs=[out, out, out],
        out_shape=[shape, shape, shape],
        compiler_params=_cparams(("arbitrary",)),
        name="rope_prep",
    )(main, main, main, *tables)


def _lam(lam_ref, lam_init):
    x = lam_ref[...]
    s1 = jnp.sum(x[0:1] * x[1:2], axis=-1, keepdims=True)
    s2 = jnp.sum(x[2:3] * x[3:4], axis=-1, keepdims=True)
    return jnp.exp(s1) - jnp.exp(s2) + lam_init


def _two_maps(q):
    lane = lax.broadcasted_iota(jnp.int32, q.shape, 1)
    zero = jnp.zeros_like(q)
    return jnp.where(lane < DIFF_QK_DIM, q, zero), jnp.where(lane >= DIFF_QK_DIM, q, zero)


def _softmax_parts(scores):
    m = functools.reduce(jnp.maximum, [jnp.max(s, axis=-1, keepdims=True) for s in scores])
    e = [jnp.exp(s - m) for s in scores]
    r = 1.0 / functools.reduce(lambda a, b: a + b, [jnp.sum(x, axis=-1, keepdims=True) for x in e])
    return [x * r for x in e]


def _diff_attend(q, keys, values, lam):
    q1, q2 = _two_maps(q)
    p1 = _softmax_parts([_mm_nt(q1, k) for k in keys])
    p2 = _softmax_parts([_mm_nt(q2, k) for k in keys])
    out = None
    for a, b, v in zip(p1, p2, values):
        o = _mm(a - lam * b, v)
        out = o if out is None else out + o
    return out


def _attn_lat_kernel(q_ref, kl_ref, vl_ref, kc_ref, vc_ref, lam_ref, gain_ref, out_ref, *, lam_init):
    lam = _lam(lam_ref, lam_init)
    o = _diff_attend(q_ref[...], [kc_ref[...].astype(BF16), kl_ref[...]],
                     [vc_ref[...].astype(BF16), vl_ref[...]], lam)
    out_ref[...] = (_rms(o) * gain_ref[...] * (1.0 - lam_init)).astype(BF16)


def _attn_ctx_kernel(q_ref, kc_ref, vc_ref, lam_ref, gain_ref, out_ref, *, lam_init):
    lam = _lam(lam_ref, lam_init)
    q = (q_ref[...] * DIFF_QK_DIM ** -0.5).astype(BF16)
    o = _diff_attend(q, [kc_ref[...].astype(BF16)], [vc_ref[...].astype(BF16)], lam)
    out_ref[...] = (_rms(o) * gain_ref[...] * (1.0 - lam_init)).astype(BF16)


def _attn_lat(qr, kr, vr, main, lam_p, gain, lam_init):
    tq = ATT_Q_TILE
    nq = SEQ // tq
    ctx_blk = LAT_ROWS // CTX_LEN
    return pl.pallas_call(
        functools.partial(_attn_lat_kernel, lam_init=lam_init),
        grid=(BATCH, DIFF_HEADS, nq),
        in_specs=[
            pl.BlockSpec((tq, LANE), lambda b, h, i: (b * nq + i, h)),
            pl.BlockSpec((SEQ, LANE), lambda b, h, i: (b, h)),
            pl.BlockSpec((SEQ, LANE), lambda b, h, i: (b, h)),
            pl.BlockSpec((CTX_LEN, LANE), lambda b, h, i: (ctx_blk + b, CK_BLK128 + h)),
            pl.BlockSpec((CTX_LEN, LANE), lambda b, h, i: (ctx_blk + b, CV_BLK128 + h)),
            pl.BlockSpec((4, DIFF_QK_DIM), lambda b, h, i: (0, 0)),
            pl.BlockSpec((1, LANE), lambda b, h, i: (0, 0)),
        ],
        out_specs=pl.BlockSpec((tq, LANE), lambda b, h, i: (b * nq + i, h)),
        out_shape=jax.ShapeDtypeStruct((LAT_ROWS, DIFF_WIDTH), BF16),
        compiler_params=_cparams(("arbitrary", "arbitrary", "arbitrary")),
        name="attn_lat",
    )(qr, kr, vr, main, main, lam_p, gain)


def _attn_ctx(main, lam_p, gain, lam_init):
    ctx_blk = LAT_ROWS // CTX_LEN
    blk = lambda c0: pl.BlockSpec((CTX_LEN, LANE), lambda b, h: (ctx_blk + b, c0 + h))
    return pl.pallas_call(
        functools.partial(_attn_ctx_kernel, lam_init=lam_init),
        grid=(BATCH, DIFF_HEADS),
        in_specs=[
            blk(CQ_BLK128), blk(CK_BLK128), blk(CV_BLK128),
            pl.BlockSpec((4, DIFF_QK_DIM), lambda b, h: (0, 0)),
            pl.BlockSpec((1, LANE), lambda b, h: (0, 0)),
        ],
        out_specs=pl.BlockSpec((CTX_LEN, LANE), lambda b, h: (b, h)),
        out_shape=jax.ShapeDtypeStruct((CTX_ROWS, DIFF_WIDTH), BF16),
        compiler_params=_cparams(("arbitrary", "arbitrary")),
        name="attn_ctx",
    )(main, main, main, lam_p, gain)


def _mixout_kernel(h_ref, gate_ref, gpost_ref, f_ref, of_ref, ob_ref, dg_ref, att_ref, dnn_ref, wo_ref,
                   out_ref, mix_scr):
    mix_scr[:, 0:FOURIER_WIDTH] = f_ref[...].astype(BF16)
    for hd in range(DN_HEADS):
        sl = slice(hd * LANE, (hd + 1) * LANE)
        o = _rms(of_ref[:, sl] + ob_ref[:, sl]) * dnn_ref[...]
        dst = slice(FOURIER_WIDTH + hd * LANE, FOURIER_WIDTH + (hd + 1) * LANE)
        mix_scr[:, dst] = (o * _silu(dg_ref[:, sl])).astype(BF16)
    mix_scr[:, FOURIER_WIDTH + DN_WIDTH:] = att_ref[...]
    m = jnp.dot(mix_scr[...], wo_ref[...], preferred_element_type=F32)
    out_ref[...] = h_ref[...] + gate_ref[...] * (_rms(m) * gpost_ref[...])


def _mixout(h, mod, gpost, f_out, o_f, o_b, main, att, dn_norm, wo):
    tm = ROW_TILE
    row = lambda w: pl.BlockSpec((tm, w), lambda i: (i, 0))
    return pl.pallas_call(
        _mixout_kernel,
        grid=(ROWS // tm,),
        in_specs=[
            row(D_MODEL),
            pl.BlockSpec((None, None, 1, D_MODEL), lambda i: (_mod_row(i, tm), 5, 0, 0)),
            pl.BlockSpec((1, D_MODEL), lambda i: (0, 0)),
            row(FOURIER_WIDTH), row(DN_WIDTH), row(DN_WIDTH),
            pl.BlockSpec((tm, DN_WIDTH), lambda i: (i, DG_BLK768)),
            row(DIFF_WIDTH),
            pl.BlockSpec((1, LANE), lambda i: (0, 0)),
            pl.BlockSpec((D_MODEL, D_MODEL), lambda i: (0, 0)),
        ],
        out_specs=row(D_MODEL),
        out_shape=jax.ShapeDtypeStruct((ROWS, D_MODEL), F32),
        scratch_shapes=[pltpu.VMEM((tm, D_MODEL), BF16)],
        compiler_params=_cparams(("arbitrary",)),
        name="mixout",
    )(h, mod, gpost, f_out, o_f, o_b, main, att, dn_norm, wo)


def _pack_ffn_in(w):
    pad = [(0, 0)] * (w.ndim - 1) + [(0, FF_PAD - D_FF)]
    return jnp.pad(w[..., :D_FF].astype(BF16), pad), jnp.pad(w[..., D_FF:].astype(BF16), pad)


def _pack_w_in(w):
    o = 0
    seg = {}
    for name, size in (("f", 512), ("qkv", 2304), ("dg", 768), ("a", 12), ("bt", 12),
                       ("cq", 768), ("ck", 768), ("cv", 768)):
        seg[name] = w[..., o:o + size]
        o += size
    lane_pad = lambda a: jnp.pad(a, [(0, 0), (0, 0), (0, LANE - a.shape[-1])])
    return jnp.concatenate([seg["qkv"], seg["cq"], seg["ck"], seg["cv"], seg["dg"],
                            lane_pad(seg["a"]), lane_pad(seg["bt"]), seg["f"]], axis=-1).astype(BF16)


def kernel(x, c, ctx, c_ctx, w_ada, b_ada, norm_pre, norm_post, ffn_w_in, ffn_w_out, w_in, dn_conv,
           dn_a_log, dn_dt_bias, dn_norm, diff_lambda, diff_norm, w_out):
    h = jnp.concatenate([x.reshape(LAT_ROWS, D_MODEL), ctx.reshape(CTX_ROWS, D_MODEL)], axis=0)

    c8 = jnp.concatenate([c, c_ctx[None, :], jnp.zeros((3, D_MODEL), F32)], axis=0)
    mod_all = _ada(c8, w_ada, b_ada).reshape(DEPTH, 8, N_MOD, 1, D_MODEL)

    wg_all, wu_all = _pack_ffn_in(ffn_w_in)
    wo_all = jnp.pad(ffn_w_out, [(0, 0), (0, 0), (0, FF_PAD - D_FF), (0, 0)]).astype(BF16)
    w_in_all = _pack_w_in(w_in)
    w_out_all = w_out.astype(BF16)
    conv_all = jnp.pad(dn_conv, [(0, 0), (0, 8 - DN_CONV), (0, 0)])
    lane_vec = lambda a: jnp.pad(a.reshape(DEPTH, 1, 2 * DN_HEADS), [(0, 0), (0, 0), (0, LANE - 2 * DN_HEADS)])
    alog_all = lane_vec(dn_a_log)
    dtb_all = lane_vec(dn_dt_bias)

    fc = _fourier_consts()
    tables = _rope_tables()
    s_zero = jnp.zeros((BATCH, 2 * DN_HEADS, DN_HEAD_DIM, DN_HEAD_DIM), F32)

    for layer in range(DEPTH):
        lam_init = 0.8 - 0.6 * math.exp(-0.3 * layer)
        mod = mod_all[layer]
        gpre = lambda s: norm_pre[layer, s].reshape(1, D_MODEL)
        gpost = lambda s: norm_post[layer, s].reshape(1, D_MODEL)

        h = _ffn(h, mod, gpre(0), gpost(0), wg_all[layer, 0], wu_all[layer, 0], wo_all[layer, 0], 0)

        main, f = _proj(h, mod, gpre(1), w_in_all[layer])
        f_out = jnp.concatenate([_fourier_lat(f, fc), _fourier_ctx(f, fc)], axis=0)
        qkv_c = _dn_prep(main, conv_all[layer], CTX_LEN, LAT_ROWS // CTX_LEN)
        qkv_l = _dn_prep(main, conv_all[layer], SEQ, 0)
        of_c, ob_c, s_ctx = _dn_scan(qkv_c, main, alog_all[layer], dtb_all[layer], s_zero, CTX_LEN,
                                     LAT_ROWS // DN_CHUNK)
        of_l, ob_l, _ = _dn_scan(qkv_l, main, alog_all[layer], dtb_all[layer], s_ctx, SEQ, 0)
        o_f = jnp.concatenate([of_l, of_c], axis=0)
        o_b = jnp.concatenate([ob_l, ob_c], axis=0)
        qr, kr, vr = _rope_prep(main, tables)
        gain = diff_norm[layer].reshape(1, DIFF_V_DIM)
        att = jnp.concatenate([_attn_lat(qr, kr, vr, main, diff_lambda[layer], gain, lam_init),
                               _attn_ctx(main, diff_lambda[layer], gain, lam_init)], axis=0)

        h = _mixout(h, mod, gpost(1), f_out, o_f, o_b, main, att,
                    dn_norm[layer].reshape(1, DN_HEAD_DIM), w_out_all[layer])
        h = _ffn(h, mod, gpre(2), gpost(2), wg_all[layer, 1], wu_all[layer, 1], wo_all[layer, 1], 2)

    return h[:LAT_ROWS].reshape(BATCH, SEQ, D_MODEL)
```
